```python
import jax, jax.numpy as jnp
from jax import lax
import numpy as np

D_MODEL = 4096
BATCH = 4
SEQ = 2048
DEPTH = 4
DEC_BATCH = 2
DEC_SEQ = 4096
PAST_LEN = 128

N_MIXERS = 2
N_A = (DEPTH + N_MIXERS - 1) // N_MIXERS
N_B = DEPTH // N_MIXERS
CHUNK = 128
SGU_HEAD_DIM = 128
SGU_GROUPS = D_MODEL // SGU_HEAD_DIM
CONV_WIDTH = 31
CONV_PAD = CONV_WIDTH // 2
D_FF = -(-8 * D_MODEL // (3 * 256)) * 256
EPS = 1e-6

kernel_name = "hybrid_sgu_conformer_conv_encoder"


def rmsnorm(x, g):
    xf = x.astype(jnp.float32)
    y = xf * lax.rsqrt(jnp.mean(xf * xf, axis=-1, keepdims=True) + EPS)
    return (y * g.astype(jnp.float32)).astype(x.dtype)


def layernorm(x, g, b):
    xf = x.astype(jnp.float32)
    mu = jnp.mean(xf, axis=-1, keepdims=True)
    xc = xf - mu
    y = xc * lax.rsqrt(jnp.mean(xc * xc, axis=-1, keepdims=True) + EPS)
    return (y * g.astype(jnp.float32) + b.astype(jnp.float32)).astype(x.dtype)


def mixer_sgu(h, w_in, w_s, b_s, g_v, w_out):
    bsz, seq, _ = h.shape
    z = jax.nn.gelu(h @ w_in, approximate=False)
    u, v = jnp.split(z, 2, axis=-1)
    v = rmsnorm(v, g_v)
    vc = v.reshape(bsz, seq // CHUNK, CHUNK, SGU_GROUPS, SGU_HEAD_DIM)
    s = jnp.einsum('gpq,bcqgd->bcpgd', w_s, vc) + b_s.T[None, None, :, :, None]
    return (u * s.reshape(bsz, seq, D_MODEL)) @ w_out


def mixer_conv(h, w_pw1, b_pw1, w_dw, b_dw, g_ln, b_ln, w_pw2):
    z = h @ w_pw1 + b_pw1
    a, gate = jnp.split(z, 2, axis=-1)
    g = a * jax.nn.sigmoid(gate)
    c = lax.conv_general_dilated(
        g, w_dw[:, None, :], window_strides=(1,), padding=[(CONV_PAD, CONV_PAD)],
        dimension_numbers=('NWC', 'WIO', 'NWC'), feature_group_count=D_MODEL) + b_dw
    c = layernorm(c, g_ln, b_ln)
    return jax.nn.silu(c) @ w_pw2


def swiglu_ffn(h, w_gate, w_up, w_down):
    return (jax.nn.silu(h @ w_gate) * (h @ w_up)) @ w_down


def trunk(x, norm_mix, norm_ffn, norm_final,
          a_w_in, a_w_s, a_b_s, a_g_v, a_w_out,
          b_w_pw1, b_b_pw1, b_w_dw, b_b_dw, b_g_ln, b_b_ln, b_w_pw2,
          ffn_w_gate, ffn_w_up, ffn_w_down):
    for i in range(DEPTH):
        h = rmsnorm(x, norm_mix[i])
        j = i // N_MIXERS
        if i % N_MIXERS == 0:
            x = x + mixer_sgu(h, a_w_in[j], a_w_s[j], a_b_s[j], a_g_v[j], a_w_out[j])
        else:
            x = x + mixer_conv(h, b_w_pw1[j], b_b_pw1[j], b_w_dw[j], b_b_dw[j],
                               b_g_ln[j], b_b_ln[j], b_w_pw2[j])
        x = x + swiglu_ffn(rmsnorm(x, norm_ffn[i]), ffn_w_gate[i], ffn_w_up[i], ffn_w_down[i])
    return rmsnorm(x, norm_final)


def setup_inputs(seed: int = 0) -> dict:
    key = jax.random.key(seed)
    ks = jax.random.split(key, 20)
    f32 = jnp.float32
    D = D_MODEL

    def nrm(k, shape, scale):
        return jax.random.normal(k, shape, f32) * scale

    return {
        "x_prompt": nrm(ks[0], (BATCH, SEQ, D), 1.0),
        "x_sample": nrm(ks[1], (DEC_BATCH, DEC_SEQ, D), 1.0),
        "norm_mix": 1.0 + nrm(ks[2], (DEPTH, D), 0.02),
        "norm_ffn": 1.0 + nrm(ks[3], (DEPTH, D), 0.02),
        "norm_final": 1.0 + nrm(ks[4], (D,), 0.02),
        "a_w_in": nrm(ks[5], (N_A, D, 2 * D), D ** -0.5),
        "a_w_s": nrm(ks[6], (N_A, SGU_GROUPS, CHUNK, CHUNK), CHUNK ** -0.5),
        "a_b_s": 1.0 + nrm(ks[7], (N_A, SGU_GROUPS, CHUNK), 0.02),
        "a_g_v": 1.0 + nrm(ks[8], (N_A, D), 0.02),
        "a_w_out": nrm(ks[9], (N_A, D, D), D ** -0.5),
        "b_w_pw1": nrm(ks[10], (N_B, D, 2 * D), D ** -0.5),
        "b_b_pw1": nrm(ks[11], (N_B, 2 * D), 0.02),
        "b_w_dw": nrm(ks[12], (N_B, CONV_WIDTH, D), CONV_WIDTH ** -0.5),
        "b_b_dw": nrm(ks[13], (N_B, D), 0.02),
        "b_g_ln": 1.0 + nrm(ks[14], (N_B, D), 0.02),
        "b_b_ln": nrm(ks[15], (N_B, D), 0.02),
        "b_w_pw2": nrm(ks[16], (N_B, D, D), D ** -0.5),
        "ffn_w_gate": nrm(ks[17], (DEPTH, D, D_FF), D ** -0.5),
        "ffn_w_up": nrm(ks[18], (DEPTH, D, D_FF), D ** -0.5),
        "ffn_w_down": nrm(ks[19], (DEPTH, D_FF, D), D_FF ** -0.5),
    }


def reference(x_prompt, x_sample, norm_mix, norm_ffn, norm_final,
              a_w_in, a_w_s, a_b_s, a_g_v, a_w_out,
              b_w_pw1, b_b_pw1, b_w_dw, b_b_dw, b_g_ln, b_b_ln, b_w_pw2,
              ffn_w_gate, ffn_w_up, ffn_w_down):
    y_prompt = trunk(x_prompt, norm_mix, norm_ffn, norm_final,
                     a_w_in, a_w_s, a_b_s, a_g_v, a_w_out,
                     b_w_pw1, b_b_pw1, b_w_dw, b_b_dw, b_g_ln, b_b_ln, b_w_pw2,
                     ffn_w_gate, ffn_w_up, ffn_w_down)
    y_sample = trunk(x_sample, norm_mix, norm_ffn, norm_final,
                     a_w_in, a_w_s, a_b_s, a_g_v, a_w_out,
                     b_w_pw1, b_b_pw1, b_w_dw, b_b_dw, b_g_ln, b_b_ln, b_w_pw2,
                     ffn_w_gate, ffn_w_up, ffn_w_down)
    return (y_prompt, y_sample)
```

```python
import functools

import jax
import jax.numpy as jnp
from jax import lax
from jax.experimental import pallas as pl
from jax.experimental.pallas import tpu as pltpu

EPS = 1e-6
CHUNK = 128
CONV_WIDTH = 31
CONV_PAD = CONV_WIDTH // 2
HALO = 16
LANES = 128
SUBLANES = 8
VMEM_LIMIT_BYTES = 56 * 1024 * 1024

F32 = jnp.float32
BF16 = jnp.bfloat16


def _params(n_axes):
    return pltpu.CompilerParams(
        dimension_semantics=("arbitrary",) * n_axes,
        vmem_limit_bytes=VMEM_LIMIT_BYTES)


def _sigmoid(x):
    return 1.0 / (1.0 + jnp.exp(-x))


def _gelu(x):
    return 0.5 * x * (1.0 + lax.erf(x * (2.0 ** -0.5)))


def _rmsnorm_kernel(x_ref, g_ref, o_ref):
    x = x_ref[...]
    r = lax.rsqrt(jnp.mean(x * x, axis=-1, keepdims=True) + EPS)
    o_ref[...] = (x * r * g_ref[...]).astype(o_ref.dtype)


def rmsnorm(x, g, out_dtype, tm=256):
    t, d = x.shape
    tm = min(tm, t)
    return pl.pallas_call(
        _rmsnorm_kernel,
        grid=(t // tm,),
        in_specs=[pl.BlockSpec((tm, d), lambda i: (i, 0)),
                  pl.BlockSpec((1, d), lambda i: (0, 0))],
        out_specs=pl.BlockSpec((tm, d), lambda i: (i, 0)),
        out_shape=jax.ShapeDtypeStruct((t, d), out_dtype),
        compiler_params=_params(1),
        name="rmsnorm",
    )(x, g.reshape(1, d).astype(F32))


def _pair_mm_kernel(a_ref, w1_ref, w2_ref, b1_ref, b2_ref, *o_refs, mode):
    a = a_ref[...]
    z1 = jnp.dot(a, w1_ref[...], preferred_element_type=F32)
    z2 = jnp.dot(a, w2_ref[...], preferred_element_type=F32)
    if mode == "gelu_pair":
        o_refs[0][...] = _gelu(z1).astype(o_refs[0].dtype)
        o_refs[1][...] = _gelu(z2).astype(o_refs[1].dtype)
    elif mode == "glu":
        z1 = z1 + b1_ref[...]
        z2 = z2 + b2_ref[...]
        o_refs[0][...] = (z1 * _sigmoid(z2)).astype(o_refs[0].dtype)
    elif mode == "swiglu":
        o_refs[0][...] = (z1 * _sigmoid(z1) * z2).astype(o_refs[0].dtype)
    else:
        raise ValueError(mode)


def pair_mm(a, w1, w2, b1, b2, *, mode, n, w2_col0, tm, tn):
    t, k = a.shape
    tm = min(tm, t)
    tn = min(tn, n)
    assert t % tm == 0 and n % tn == 0 and w2_col0 % tn == 0
    off = w2_col0 // tn
    n_out = 2 if mode == "gelu_pair" else 1
    out_spec = pl.BlockSpec((tm, tn), lambda i, j: (i, j))
    out_shape = jax.ShapeDtypeStruct((t, n), BF16)
    res = pl.pallas_call(
        functools.partial(_pair_mm_kernel, mode=mode),
        grid=(t // tm, n // tn),
        in_specs=[pl.BlockSpec((tm, k), lambda i, j: (i, 0)),
                  pl.BlockSpec((k, tn), lambda i, j: (0, j)),
                  pl.BlockSpec((k, tn), lambda i, j: (0, j + off)),
                  pl.BlockSpec((1, tn), lambda i, j: (0, j)),
                  pl.BlockSpec((1, tn), lambda i, j: (0, j + off))],
        out_specs=[out_spec] * n_out,
        out_shape=[out_shape] * n_out,
        compiler_params=_params(2),
        name="pair_mm_" + mode,
    )(a, w1, w2, b1, b2)
    return res if n_out == 2 else res[0]


def _res_mm_kernel(a_ref, w_ref, r_ref, o_ref, *, nk):
    prod = jnp.dot(a_ref[...], w_ref[...], preferred_element_type=F32)
    if nk == 1:
        o_ref[...] = r_ref[...] + prod
    else:
        k = pl.program_id(2)

        @pl.when(k == 0)
        def _():
            o_ref[...] = r_ref[...] + prod

        @pl.when(k > 0)
        def _():
            o_ref[...] += prod


def res_mm(a, w, resid, *, tm, tn, tk):
    t, k = a.shape
    n = w.shape[1]
    tm, tn, tk = min(tm, t), min(tn, n), min(tk, k)
    assert t % tm == 0 and n % tn == 0 and k % tk == 0
    nk = k // tk
    return pl.pallas_call(
        functools.partial(_res_mm_kernel, nk=nk),
        grid=(t // tm, n // tn, nk),
        in_specs=[pl.BlockSpec((tm, tk), lambda i, j, kk: (i, kk)),
                  pl.BlockSpec((tk, tn), lambda i, j, kk: (kk, j)),
                  pl.BlockSpec((tm, tn), lambda i, j, kk: (i, j))],
        out_specs=pl.BlockSpec((tm, tn), lambda i, j, kk: (i, j)),
        out_shape=jax.ShapeDtypeStruct((t, n), F32),
        compiler_params=_params(3),
        name="res_mm",
    )(a, w, resid)


def _sgu_kernel(u_ref, v_ref, ws_ref, bs_ref, gv_ref, o_ref, vn_ref, *, n_chunks, n_groups):
    for c in range(n_chunks):
        rows = slice(c * CHUNK, (c + 1) * CHUNK)
        vf = v_ref[rows, :].astype(F32)
        r = lax.rsqrt(jnp.mean(vf * vf, axis=-1, keepdims=True) + EPS)
        vn_ref[rows, :] = (vf * r * gv_ref[...]).astype(vn_ref.dtype)
    for g in range(n_groups):
        cols = slice(g * CHUNK, (g + 1) * CHUNK)
        rhs = jnp.concatenate(
            [vn_ref[c * CHUNK:(c + 1) * CHUNK, cols] for c in range(n_chunks)], axis=1)
        s = jnp.dot(ws_ref[g], rhs, preferred_element_type=F32)
        for c in range(n_chunks):
            rows = slice(c * CHUNK, (c + 1) * CHUNK)
            sc = s[:, c * CHUNK:(c + 1) * CHUNK] + bs_ref[:, cols]
            o_ref[rows, cols] = (u_ref[rows, cols].astype(F32) * sc).astype(o_ref.dtype)


def sgu_gate(u, v, w_s, b_full, g_v, *, tm):
    t, d = u.shape
    n_groups = w_s.shape[0]
    tm = min(tm, t)
    assert tm % CHUNK == 0 and t % tm == 0 and d == n_groups * CHUNK
    row_spec = pl.BlockSpec((tm, d), lambda i: (i, 0))
    return pl.pallas_call(
        functools.partial(_sgu_kernel, n_chunks=tm // CHUNK, n_groups=n_groups),
        grid=(t // tm,),
        in_specs=[row_spec, row_spec,
                  pl.BlockSpec((n_groups, CHUNK, CHUNK), lambda i: (0, 0, 0)),
                  pl.BlockSpec((CHUNK, d), lambda i: (0, 0)),
                  pl.BlockSpec((1, d), lambda i: (0, 0))],
        out_specs=row_spec,
        out_shape=jax.ShapeDtypeStruct((t, d), BF16),
        scratch_shapes=[pltpu.VMEM((tm, d), BF16)],
        compiler_params=_params(1),
        name="sgu_gate",
    )(u, v, w_s, b_full, g_v.reshape(1, d).astype(F32))


CONV_ROWS = 64
LN_ROWS = 16


def _conv_kernel(g_ref, gp_ref, gn_ref, w_ref, bdw_ref, gln_ref, bln_ref, o_ref,
                 ext_ref, c_ref, *, tm, d, t_first, seq_first, seq_second):
    row0 = pl.program_id(0) * tm
    in_first = row0 < t_first
    seq = jnp.where(in_first, seq_first, seq_second)
    off = jnp.where(in_first, row0, row0 - t_first)
    at_start = lax.rem(off, seq) == 0
    at_end = lax.rem(off + tm, seq) == 0

    ext_ref[HALO:HALO + tm, :] = g_ref[...].astype(F32)
    ext_ref[0:HALO, :] = jnp.where(at_start, 0.0, gp_ref[...].astype(F32))
    ext_ref[HALO + tm:2 * HALO + tm, :] = jnp.where(at_end, 0.0, gn_ref[...].astype(F32))

    win_rows = CONV_ROWS + 2 * HALO
    shift0 = HALO - CONV_PAD

    for cs in range(d // LANES):
        cols = slice(cs * LANES, (cs + 1) * LANES)

        def conv_step(rc, carry, cols=cols):
            base = pl.multiple_of(rc * CONV_ROWS, CONV_ROWS)
            win = ext_ref[pl.ds(base, win_rows), cols]
            acc = jnp.zeros((CONV_ROWS, LANES), F32)
            for b in range(SUBLANES):
                wb = win if b == 0 else pltpu.roll(win, win_rows - b, axis=0)
                for a in range((shift0 + CONV_WIDTH + SUBLANES - 1) // SUBLANES):
                    k = SUBLANES * a + b - shift0
                    if 0 <= k < CONV_WIDTH:
                        acc = acc + w_ref[k:k + 1, cols] * wb[SUBLANES * a:SUBLANES * a + CONV_ROWS, :]
            c_ref[pl.ds(base, CONV_ROWS), cols] = acc + bdw_ref[:, cols]
            return carry

        lax.fori_loop(0, tm // CONV_ROWS, conv_step, 0)

    def ln_step(rc, carry):
        base = pl.multiple_of(rc * LN_ROWS, LN_ROWS)
        c = c_ref[pl.ds(base, LN_ROWS), :]
        mu = jnp.mean(c, axis=-1, keepdims=True)
        xc = c - mu
        y = xc * lax.rsqrt(jnp.mean(xc * xc, axis=-1, keepdims=True) + EPS)
        y = y * gln_ref[...] + bln_ref[...]
        o_ref[pl.ds(base, LN_ROWS), :] = (y * _sigmoid(y)).astype(o_ref.dtype)
        return carry

    lax.fori_loop(0, tm // LN_ROWS, ln_step, 0)


def conv_module(g, w_dw, b_dw, g_ln, b_ln, *, tm, t_first, seq_first, seq_second):
    t, d = g.shape
    tm = min(tm, seq_first, seq_second)
    assert t % tm == 0 and seq_first % tm == 0 and seq_second % tm == 0 and t_first % tm == 0
    assert tm % CONV_ROWS == 0 and tm % HALO == 0 and d % LANES == 0
    hb = tm // HALO
    n_halo_blocks = t // HALO
    vec = lambda x: x.reshape(1, d).astype(F32)
    vec_spec = pl.BlockSpec((1, d), lambda i: (0, 0))
    return pl.pallas_call(
        functools.partial(_conv_kernel, tm=tm, d=d, t_first=t_first,
                          seq_first=seq_first, seq_second=seq_second),
        grid=(t // tm,),
        in_specs=[pl.BlockSpec((tm, d), lambda i: (i, 0)),
                  pl.BlockSpec((HALO, d), lambda i: (jnp.maximum(i * hb - 1, 0), 0)),
                  pl.BlockSpec((HALO, d), lambda i: (jnp.minimum((i + 1) * hb, n_halo_blocks - 1), 0)),
                  pl.BlockSpec((CONV_WIDTH, d), lambda i: (0, 0)),
                  vec_spec, vec_spec, vec_spec],
        out_specs=pl.BlockSpec((tm, d), lambda i: (i, 0)),
        out_shape=jax.ShapeDtypeStruct((t, d), BF16),
        scratch_shapes=[pltpu.VMEM((tm + 2 * HALO, d), F32), pltpu.VMEM((tm, d), F32)],
        compiler_params=_params(1),
        name="conv_module",
    )(g, g, g, w_dw.astype(F32), vec(b_dw), vec(g_ln), vec(b_ln))


def _trunk(x, t_first, seq_first, seq_second, norm_mix, norm_ffn, norm_final,
           a_w_in, a_w_s, a_b_s, a_g_v, a_w_out,
           b_w_pw1, b_b_pw1, b_w_dw, b_b_dw, b_g_ln, b_b_ln, b_w_pw2,
           ffn_w_gate, ffn_w_up, ffn_w_down):
    d = x.shape[1]
    depth = norm_mix.shape[0]
    n_mixers = 2
    zero_bias = jnp.zeros((1, 2 * d), F32)
    for i in range(depth):
        h = rmsnorm(x, norm_mix[i], BF16)
        j = i // n_mixers
        if i % n_mixers == 0:
            w_in = a_w_in[j].astype(BF16)
            u, v = pair_mm(h, w_in, w_in, zero_bias, zero_bias, mode="gelu_pair",
                           n=d, w2_col0=d, tm=1024, tn=512)
            b_full = jnp.repeat(a_b_s[j].T.astype(F32), CHUNK, axis=1)
            us = sgu_gate(u, v, a_w_s[j].astype(BF16), b_full, a_g_v[j], tm=512)
            x = res_mm(us, a_w_out[j].astype(BF16), x, tm=1024, tn=512, tk=d)
        else:
            w_pw1 = b_w_pw1[j].astype(BF16)
            bias = b_b_pw1[j].reshape(1, 2 * d).astype(F32)
            g = pair_mm(h, w_pw1, w_pw1, bias, bias, mode="glu",
                        n=d, w2_col0=d, tm=1024, tn=512)
            c = conv_module(g, b_w_dw[j], b_b_dw[j], b_g_ln[j], b_b_ln[j], tm=256,
                            t_first=t_first, seq_first=seq_first, seq_second=seq_second)
            x = res_mm(c, b_w_pw2[j].astype(BF16), x, tm=1024, tn=512, tk=d)
        hn = rmsnorm(x, norm_ffn[i], BF16)
        d_ff = ffn_w_gate.shape[2]
        hh = pair_mm(hn, ffn_w_gate[i].astype(BF16), ffn_w_up[i].astype(BF16),
                     zero_bias, zero_bias, mode="swiglu", n=d_ff, w2_col0=0, tm=2048, tn=256)
        x = res_mm(hh, ffn_w_down[i].astype(BF16), x, tm=1024, tn=512, tk=d_ff // 2)
    return rmsnorm(x, norm_final, F32)


def kernel(x_prompt, x_sample, norm_mix, norm_ffn, norm_final, a_w_in, a_w_s, a_b_s, a_g_v, a_w_out, b_w_pw1, b_b_pw1, b_w_dw, b_b_dw, b_g_ln, b_b_ln, b_w_pw2, ffn_w_gate, ffn_w_up, ffn_w_down):
    d = x_prompt.shape[-1]
    t_first = x_prompt.shape[0] * x_prompt.shape[1]
    x = jnp.concatenate([x_prompt.reshape(-1, d), x_sample.reshape(-1, d)], axis=0)
    y = _trunk(x, t_first, x_prompt.shape[1], x_sample.shape[1], norm_mix, norm_ffn, norm_final,
               a_w_in, a_w_s, a_b_s, a_g_v, a_w_out,
               b_w_pw1, b_b_pw1, b_w_dw, b_b_dw, b_g_ln, b_b_ln, b_w_pw2,
               ffn_w_gate, ffn_w_up, ffn_w_down)
    return (y[:t_first].reshape(x_prompt.shape), y[t_first:].reshape(x_sample.shape))
```

```python
import functools

import jax
import jax.numpy as jnp
from jax import lax
from jax.experimental import pallas as pl
from jax.experimental.pallas import tpu as pltpu

EPS = 1e-6
CHUNK = 128
CONV_WIDTH = 31
CONV_PAD = CONV_WIDTH // 2
HALO = 16
LANES = 128
SUBLANES = 8
VMEM_LIMIT_BYTES = 56 * 1024 * 1024

F32 = jnp.float32
BF16 = jnp.bfloat16


def _params(n_axes):
    return pltpu.CompilerParams(
        dimension_semantics=("arbitrary",) * n_axes,
        vmem_limit_bytes=VMEM_LIMIT_BYTES)


def _sigmoid(x):
    return 1.0 / (1.0 + jnp.exp(-x))


def _gelu(x):
    return 0.5 * x * (1.0 + lax.erf(x * (2.0 ** -0.5)))


def _rstd(sum_sq, d):
    return lax.rsqrt(sum_sq / d + EPS)


def _enter_kernel(x_ref, xb_ref, rstd_ref):
    x = x_ref[...]
    xb_ref[...] = x.astype(xb_ref.dtype)
    rstd_ref[...] = _rstd(jnp.sum(x * x, axis=-1, keepdims=True), x.shape[-1])


def enter_stream(x, tm=256):
    t, d = x.shape
    tm = min(tm, t)
    return pl.pallas_call(
        _enter_kernel,
        grid=(t // tm,),
        in_specs=[pl.BlockSpec((tm, d), lambda i: (i, 0))],
        out_specs=[pl.BlockSpec((tm, d), lambda i: (i, 0)),
                   pl.BlockSpec((tm, 1), lambda i: (i, 0))],
        out_shape=[jax.ShapeDtypeStruct((t, d), BF16), jax.ShapeDtypeStruct((t, 1), F32)],
        compiler_params=_params(1),
        name="enter_stream",
    )(x)


def _rmsnorm_kernel(x_ref, g_ref, o_ref):
    x = x_ref[...]
    r = lax.rsqrt(jnp.mean(x * x, axis=-1, keepdims=True) + EPS)
    o_ref[...] = (x * r * g_ref[...]).astype(o_ref.dtype)


def rmsnorm(x, g, out_dtype, tm=256):
    t, d = x.shape
    tm = min(tm, t)
    return pl.pallas_call(
        _rmsnorm_kernel,
        grid=(t // tm,),
        in_specs=[pl.BlockSpec((tm, d), lambda i: (i, 0)),
                  pl.BlockSpec((1, d), lambda i: (0, 0))],
        out_specs=pl.BlockSpec((tm, d), lambda i: (i, 0)),
        out_shape=jax.ShapeDtypeStruct((t, d), out_dtype),
        compiler_params=_params(1),
        name="rmsnorm",
    )(x, g.reshape(1, d).astype(F32))


def _pair_mm_kernel(*refs, mode):
    if mode == "glu":
        a_ref, rstd_ref, w1_ref, w2_ref, b1_ref, b2_ref, *o_refs = refs
    else:
        a_ref, rstd_ref, w1_ref, w2_ref, *o_refs = refs
    a = a_ref[...]
    rstd = rstd_ref[...]
    z1 = jnp.dot(a, w1_ref[...], preferred_element_type=F32) * rstd
    z2 = jnp.dot(a, w2_ref[...], preferred_element_type=F32) * rstd
    if mode == "gelu_pair":
        o_refs[0][...] = _gelu(z1).astype(o_refs[0].dtype)
        o_refs[1][...] = _gelu(z2).astype(o_refs[1].dtype)
    elif mode == "glu":
        z1 = z1 + b1_ref[...]
        z2 = z2 + b2_ref[...]
        o_refs[0][...] = (z1 * _sigmoid(z2)).astype(o_refs[0].dtype)
    elif mode == "swiglu":
        o_refs[0][...] = (z1 * _sigmoid(z1) * z2).astype(o_refs[0].dtype)
    else:
        raise ValueError(mode)


def pair_mm(a, rstd, w1, w2, bias, *, mode, n, w2_col0, tm, tn):
    t, k = a.shape
    tm = min(tm, t)
    tn = min(tn, n)
    assert t % tm == 0 and n % tn == 0 and w2_col0 % tn == 0
    off = w2_col0 // tn
    n_out = 2 if mode == "gelu_pair" else 1
    in_specs = [pl.BlockSpec((tm, k), lambda i, j: (i, 0)),
                pl.BlockSpec((tm, 1), lambda i, j: (i, 0)),
                pl.BlockSpec((k, tn), lambda i, j: (0, j)),
                pl.BlockSpec((k, tn), lambda i, j: (0, j + off))]
    args = [a, rstd, w1, w2]
    if mode == "glu":
        in_specs += [pl.BlockSpec((1, tn), lambda i, j: (0, j)),
                     pl.BlockSpec((1, tn), lambda i, j: (0, j + off))]
        args += [bias, bias]
    out_spec = pl.BlockSpec((tm, tn), lambda i, j: (i, j))
    out_shape = jax.ShapeDtypeStruct((t, n), BF16)
    res = pl.pallas_call(
        functools.partial(_pair_mm_kernel, mode=mode),
        grid=(t // tm, n // tn),
        in_specs=in_specs,
        out_specs=[out_spec] * n_out,
        out_shape=[out_shape] * n_out,
        compiler_params=_params(2),
        name="pair_mm_" + mode,
    )(*args)
    return res if n_out == 2 else res[0]


def _res_mm_kernel(a_ref, w_ref, r_ref, o_ref, ob_ref, rstd_ref, ss_ref, *, nj, nk, d):
    j = pl.program_id(1)
    prod = jnp.dot(a_ref[...], w_ref[...], preferred_element_type=F32)

    def finish(x_new):
        o_ref[...] = x_new
        ob_ref[...] = x_new.astype(ob_ref.dtype)
        part = jnp.sum(x_new * x_new, axis=-1, keepdims=True)

        @pl.when(j == 0)
        def _():
            ss_ref[...] = part

        @pl.when(j > 0)
        def _():
            ss_ref[...] += part

        @pl.when(j == nj - 1)
        def _():
            rstd_ref[...] = _rstd(ss_ref[...], d)

    if nk == 1:
        finish(r_ref[...] + prod)
    else:
        k = pl.program_id(2)

        @pl.when(k == 0)
        def _():
            o_ref[...] = r_ref[...] + prod

        @pl.when(jnp.logical_and(k > 0, k < nk - 1))
        def _():
            o_ref[...] += prod

        @pl.when(k == nk - 1)
        def _():
            finish(o_ref[...] + prod)


def res_mm(a, w, resid, *, tm, tn, tk):
    t, k = a.shape
    n = w.shape[1]
    tm, tn, tk = min(tm, t), min(tn, n), min(tk, k)
    assert t % tm == 0 and n % tn == 0 and k % tk == 0
    nj, nk = n // tn, k // tk
    tile = pl.BlockSpec((tm, tn), lambda i, j, kk: (i, j))
    return pl.pallas_call(
        functools.partial(_res_mm_kernel, nj=nj, nk=nk, d=n),
        grid=(t // tm, nj, nk),
        in_specs=[pl.BlockSpec((tm, tk), lambda i, j, kk: (i, kk)),
                  pl.BlockSpec((tk, tn), lambda i, j, kk: (kk, j)),
                  tile],
        out_specs=[tile, tile, pl.BlockSpec((tm, 1), lambda i, j, kk: (i, 0))],
        out_shape=[jax.ShapeDtypeStruct((t, n), F32), jax.ShapeDtypeStruct((t, n), BF16),
                   jax.ShapeDtypeStruct((t, 1), F32)],
        scratch_shapes=[pltpu.VMEM((tm, 1), F32)],
        compiler_params=_params(3),
        name="res_mm",
    )(a, w, resid)


def _sgu_kernel(u_ref, v_ref, ws_ref, bs_ref, gv_ref, o_ref, vn_ref, *, n_chunks, n_groups):
    for c in range(n_chunks):
        rows = slice(c * CHUNK, (c + 1) * CHUNK)
        vf = v_ref[rows, :].astype(F32)
        r = lax.rsqrt(jnp.mean(vf * vf, axis=-1, keepdims=True) + EPS)
        vn_ref[rows, :] = (vf * r * gv_ref[...]).astype(vn_ref.dtype)
    for g in range(n_groups):
        cols = slice(g * CHUNK, (g + 1) * CHUNK)
        rhs = jnp.concatenate(
            [vn_ref[c * CHUNK:(c + 1) * CHUNK, cols] for c in range(n_chunks)], axis=1)
        s = jnp.dot(ws_ref[g], rhs, preferred_element_type=F32)
        for c in range(n_chunks):
            rows = slice(c * CHUNK, (c + 1) * CHUNK)
            sc = s[:, c * CHUNK:(c + 1) * CHUNK] + bs_ref[:, cols]
            o_ref[rows, cols] = (u_ref[rows, cols].astype(F32) * sc).astype(o_ref.dtype)


def sgu_gate(u, v, w_s, b_full, g_v, *, tm):
    t, d = u.shape
    n_groups = w_s.shape[0]
    tm = min(tm, t)
    assert tm % CHUNK == 0 and t % tm == 0 and d == n_groups * CHUNK
    row_spec = pl.BlockSpec((tm, d), lambda i: (i, 0))
    return pl.pallas_call(
        functools.partial(_sgu_kernel, n_chunks=tm // CHUNK, n_groups=n_groups),
        grid=(t // tm,),
        in_specs=[row_spec, row_spec,
                  pl.BlockSpec((n_groups, CHUNK, CHUNK), lambda i: (0, 0, 0)),
                  pl.BlockSpec((CHUNK, d), lambda i: (0, 0)),
                  pl.BlockSpec((1, d), lambda i: (0, 0))],
        out_specs=row_spec,
        out_shape=jax.ShapeDtypeStruct((t, d), BF16),
        scratch_shapes=[pltpu.VMEM((tm, d), BF16)],
        compiler_params=_params(1),
        name="sgu_gate",
    )(u, v, w_s, b_full, g_v.reshape(1, d).astype(F32))


CONV_ROWS = 64
CONV_COLS = LANES
WIN_ROWS = CONV_ROWS + 2 * HALO
TAP0 = HALO - CONV_PAD
LN_ROWS = 64
LN_APPLY_ROWS = 16


def _conv_kernel(g_ref, gp_ref, gn_ref, w_ref, bdw_ref, gln_ref, bln_ref, o_ref,
                 ext_ref, c_ref, mu_ref, rs_ref, *, tm, d, seq):
    row0 = pl.program_id(0) * tm
    at_start = lax.rem(row0, seq) == 0
    at_end = lax.rem(row0 + tm, seq) == 0

    ext_ref[HALO:HALO + tm, :] = g_ref[...].astype(F32)
    ext_ref[0:HALO, :] = jnp.where(at_start, 0.0, gp_ref[...].astype(F32))
    ext_ref[HALO + tm:2 * HALO + tm, :] = jnp.where(at_end, 0.0, gn_ref[...].astype(F32))

    for cs in range(d // CONV_COLS):
        cols = slice(cs * CONV_COLS, (cs + 1) * CONV_COLS)

        def conv_step(rc, carry, cols=cols):
            base = pl.multiple_of(rc * CONV_ROWS, CONV_ROWS)
            win = ext_ref[pl.ds(base, WIN_ROWS), cols]
            acc = jnp.zeros((CONV_ROWS, CONV_COLS), F32)
            for b in range(SUBLANES):
                wb = win if b == 0 else pltpu.roll(win, WIN_ROWS - b, axis=0)
                for m in range(b, TAP0 + CONV_WIDTH, SUBLANES):
                    k = m - TAP0
                    if k >= 0:
                        acc = acc + w_ref[k:k + 1, cols] * wb[m - b:m - b + CONV_ROWS, :]
            c_ref[pl.ds(base, CONV_ROWS), cols] = acc + bdw_ref[:, cols]
            return carry

        lax.fori_loop(0, tm // CONV_ROWS, conv_step, 0)

    def rows_of(rc, n):
        return pl.ds(pl.multiple_of(rc * n, n), n)

    def mean_step(rc, carry):
        rows = rows_of(rc, LN_ROWS)
        mu_ref[rows, :] = jnp.mean(c_ref[rows, :], axis=-1, keepdims=True)
        return carry

    def var_step(rc, carry):
        rows = rows_of(rc, LN_ROWS)
        xc = c_ref[rows, :] - mu_ref[rows, :]
        rs_ref[rows, :] = lax.rsqrt(jnp.mean(xc * xc, axis=-1, keepdims=True) + EPS)
        return carry

    def apply_step(rc, carry):
        rows = rows_of(rc, LN_APPLY_ROWS)
        y = (c_ref[rows, :] - mu_ref[rows, :]) * rs_ref[rows, :]
        y = y * gln_ref[...] + bln_ref[...]
        o_ref[rows, :] = (y * _sigmoid(y)).astype(o_ref.dtype)
        return carry

    lax.fori_loop(0, tm // LN_ROWS, mean_step, 0)
    lax.fori_loop(0, tm // LN_ROWS, var_step, 0)
    lax.fori_loop(0, tm // LN_APPLY_ROWS, apply_step, 0)


def conv_module(g, w_dw, b_dw, g_ln, b_ln, *, tm, seq):
    t, d = g.shape
    tm = min(tm, seq)
    assert t % seq == 0 and seq % tm == 0
    assert tm % CONV_ROWS == 0 and tm % LN_ROWS == 0 and tm % HALO == 0 and d % CONV_COLS == 0
    hb = tm // HALO
    n_halo_blocks = t // HALO
    vec = lambda x: x.reshape(1, d).astype(F32)
    vec_spec = pl.BlockSpec((1, d), lambda i: (0, 0))
    return pl.pallas_call(
        functools.partial(_conv_kernel, tm=tm, d=d, seq=seq),
        grid=(t // tm,),
        in_specs=[pl.BlockSpec((tm, d), lambda i: (i, 0)),
                  pl.BlockSpec((HALO, d), lambda i: (jnp.maximum(i * hb - 1, 0), 0)),
                  pl.BlockSpec((HALO, d), lambda i: (jnp.minimum((i + 1) * hb, n_halo_blocks - 1), 0)),
                  pl.BlockSpec((CONV_WIDTH, d), lambda i: (0, 0)),
                  vec_spec, vec_spec, vec_spec],
        out_specs=pl.BlockSpec((tm, d), lambda i: (i, 0)),
        out_shape=jax.ShapeDtypeStruct((t, d), BF16),
        scratch_shapes=[pltpu.VMEM((tm + 2 * HALO, d), F32), pltpu.VMEM((tm, d), F32),
                        pltpu.VMEM((tm, 1), F32), pltpu.VMEM((tm, 1), F32)],
        compiler_params=_params(1),
        name="conv_module",
    )(g, g, g, w_dw.astype(F32), vec(b_dw), vec(g_ln), vec(b_ln))


N_MIXERS = 2


def _prepare_weights(norm_mix, norm_ffn, a_w_in, a_w_s, a_b_s, a_w_out,
                     b_w_pw1, b_b_pw1, b_w_pw2, ffn_w_gate, ffn_w_up, ffn_w_down):
    depth = norm_mix.shape[0]
    layers = []
    for i in range(depth):
        j = i // N_MIXERS
        gm = norm_mix[i].astype(F32)[:, None]
        gf = norm_ffn[i].astype(F32)[:, None]
        lw = {
            "w_gate": (gf * ffn_w_gate[i]).astype(BF16),
            "w_up": (gf * ffn_w_up[i]).astype(BF16),
            "w_down": ffn_w_down[i].astype(BF16),
        }
        if i % N_MIXERS == 0:
            lw["w_in"] = (gm * a_w_in[j]).astype(BF16)
            lw["w_s"] = a_w_s[j].astype(BF16)
            lw["b_full"] = jnp.repeat(a_b_s[j].T.astype(F32), CHUNK, axis=1)
            lw["w_out"] = a_w_out[j].astype(BF16)
        else:
            lw["w_pw1"] = (gm * b_w_pw1[j]).astype(BF16)
            lw["b_pw1"] = b_b_pw1[j].reshape(1, -1).astype(F32)
            lw["w_pw2"] = b_w_pw2[j].astype(BF16)
        layers.append(lw)
    return layers


def _trunk(x, seq, layers, norm_final, a_g_v, b_w_dw, b_b_dw, b_g_ln, b_b_ln):
    d = x.shape[1]
    xb, rstd = enter_stream(x)
    for i, lw in enumerate(layers):
        j = i // N_MIXERS
        if i % N_MIXERS == 0:
            u, v = pair_mm(xb, rstd, lw["w_in"], lw["w_in"], None, mode="gelu_pair",
                           n=d, w2_col0=d, tm=1024, tn=512)
            us = sgu_gate(u, v, lw["w_s"], lw["b_full"], a_g_v[j], tm=512)
            x, xb, rstd = res_mm(us, lw["w_out"], x, tm=1024, tn=512, tk=d)
        else:
            g = pair_mm(xb, rstd, lw["w_pw1"], lw["w_pw1"], lw["b_pw1"], mode="glu",
                        n=d, w2_col0=d, tm=1024, tn=512)
            c = conv_module(g, b_w_dw[j], b_b_dw[j], b_g_ln[j], b_b_ln[j], tm=256, seq=seq)
            x, xb, rstd = res_mm(c, lw["w_pw2"], x, tm=1024, tn=512, tk=d)
        d_ff = lw["w_gate"].shape[1]
        hh = pair_mm(xb, rstd, lw["w_gate"], lw["w_up"], None, mode="swiglu",
                     n=d_ff, w2_col0=0, tm=2048, tn=256)
        x, xb, rstd = res_mm(hh, lw["w_down"], x, tm=1024, tn=512, tk=d_ff // 2)
    return rmsnorm(x, norm_final, F32)


def kernel(x_prompt, x_sample, norm_mix, norm_ffn, norm_final, a_w_in, a_w_s, a_b_s, a_g_v, a_w_out, b_w_pw1, b_b_pw1, b_w_dw, b_b_dw, b_g_ln, b_b_ln, b_w_pw2, ffn_w_gate, ffn_w_up, ffn_w_down):
    layers = _prepare_weights(norm_mix, norm_ffn, a_w_in, a_w_s, a_b_s, a_w_out,
                              b_w_pw1, b_b_pw1, b_w_pw2, ffn_w_gate, ffn_w_up, ffn_w_down)
    outs = []
    for x in (x_prompt, x_sample):
        d = x.shape[-1]
        y = _trunk(x.reshape(-1, d), x.shape[1], layers, norm_final,
                   a_g_v, b_w_dw, b_b_dw, b_g_ln, b_b_ln)
        outs.append(y.reshape(x.shape))
    return tuple(outs)
```

```python
import functools

import jax
import jax.numpy as jnp
from jax import lax
from jax.experimental import pallas as pl
from jax.experimental.pallas import tpu as pltpu

EPS = 1e-6
CHUNK = 128
CONV_WIDTH = 31
CONV_PAD = CONV_WIDTH // 2
HALO = 16
LANES = 128
SUBLANES = 8
VMEM_LIMIT_BYTES = 56 * 1024 * 1024

F32 = jnp.float32
BF16 = jnp.bfloat16


def _params(n_axes):
    return pltpu.CompilerParams(
        dimension_semantics=("arbitrary",) * n_axes,
        vmem_limit_bytes=VMEM_LIMIT_BYTES)


def _sigmoid(x):
    return 1.0 / (1.0 + jnp.exp(-x))


def _gelu(x):
    return 0.5 * x * (1.0 + lax.erf(x * (2.0 ** -0.5)))


def _rstd(sum_sq, d):
    return lax.rsqrt(sum_sq / d + EPS)


def _enter_kernel(x_ref, xb_ref, rstd_ref):
    x = x_ref[...]
    xb_ref[...] = x.astype(xb_ref.dtype)
    rstd_ref[...] = _rstd(jnp.sum(x * x, axis=-1, keepdims=True), x.shape[-1])


def enter_stream(x, tm=256):
    t, d = x.shape
    tm = min(tm, t)
    return pl.pallas_call(
        _enter_kernel,
        grid=(t // tm,),
        in_specs=[pl.BlockSpec((tm, d), lambda i: (i, 0))],
        out_specs=[pl.BlockSpec((tm, d), lambda i: (i, 0)),
                   pl.BlockSpec((tm, 1), lambda i: (i, 0))],
        out_shape=[jax.ShapeDtypeStruct((t, d), BF16), jax.ShapeDtypeStruct((t, 1), F32)],
        compiler_params=_params(1),
        name="enter_stream",
    )(x)


def _rmsnorm_kernel(x_ref, g_ref, o_ref):
    x = x_ref[...]
    r = lax.rsqrt(jnp.mean(x * x, axis=-1, keepdims=True) + EPS)
    o_ref[...] = (x * r * g_ref[...]).astype(o_ref.dtype)


def rmsnorm(x, g, out_dtype, tm=256):
    t, d = x.shape
    tm = min(tm, t)
    return pl.pallas_call(
        _rmsnorm_kernel,
        grid=(t // tm,),
        in_specs=[pl.BlockSpec((tm, d), lambda i: (i, 0)),
                  pl.BlockSpec((1, d), lambda i: (0, 0))],
        out_specs=pl.BlockSpec((tm, d), lambda i: (i, 0)),
        out_shape=jax.ShapeDtypeStruct((t, d), out_dtype),
        compiler_params=_params(1),
        name="rmsnorm",
    )(x, g.reshape(1, d).astype(F32))


PAIR_SUB_ROWS = 512


def _pair_mm_kernel(*refs, mode):
    if mode == "glu":
        a_ref, rstd_ref, w1_ref, w2_ref, b1_ref, b2_ref, *o_refs = refs
    else:
        a_ref, rstd_ref, w1_ref, w2_ref, *o_refs = refs
    tm = a_ref.shape[0]
    sub = min(PAIR_SUB_ROWS, tm)
    for r in range(tm // sub):
        rows = slice(r * sub, (r + 1) * sub)
        a = a_ref[rows, :]
        rstd = rstd_ref[rows, :]
        z1 = jnp.dot(a, w1_ref[...], preferred_element_type=F32) * rstd
        z2 = jnp.dot(a, w2_ref[...], preferred_element_type=F32) * rstd
        if mode == "gelu_pair":
            o_refs[0][rows, :] = _gelu(z1).astype(o_refs[0].dtype)
            o_refs[1][rows, :] = _gelu(z2).astype(o_refs[1].dtype)
        elif mode == "glu":
            z1 = z1 + b1_ref[...]
            z2 = z2 + b2_ref[...]
            o_refs[0][rows, :] = (z1 * _sigmoid(z2)).astype(o_refs[0].dtype)
        elif mode == "swiglu":
            o_refs[0][rows, :] = (z1 * _sigmoid(z1) * z2).astype(o_refs[0].dtype)
        else:
            raise ValueError(mode)


def pair_mm(a, rstd, w1, w2, bias, *, layer, mode, n, w2_col0, tm, tn):
    t, k = a.shape
    tm = min(tm, t)
    tn = min(tn, n)
    assert t % tm == 0 and n % tn == 0 and w2_col0 % tn == 0
    off = w2_col0 // tn
    n_out = 2 if mode == "gelu_pair" else 1
    in_specs = [pl.BlockSpec((tm, k), lambda i, j: (i, 0)),
                pl.BlockSpec((tm, 1), lambda i, j: (i, 0)),
                pl.BlockSpec((None, k, tn), lambda i, j: (layer, 0, j)),
                pl.BlockSpec((None, k, tn), lambda i, j: (layer, 0, j + off))]
    args = [a, rstd, w1, w2]
    if mode == "glu":
        in_specs += [pl.BlockSpec((1, tn), lambda i, j: (0, j)),
                     pl.BlockSpec((1, tn), lambda i, j: (0, j + off))]
        args += [bias, bias]
    out_spec = pl.BlockSpec((tm, tn), lambda i, j: (i, j))
    out_shape = jax.ShapeDtypeStruct((t, n), BF16)
    res = pl.pallas_call(
        functools.partial(_pair_mm_kernel, mode=mode),
        grid=(t // tm, n // tn),
        in_specs=in_specs,
        out_specs=[out_spec] * n_out,
        out_shape=[out_shape] * n_out,
        compiler_params=_params(2),
        name="pair_mm_" + mode,
    )(*args)
    return res if n_out == 2 else res[0]


RES_SUB_ROWS = 256


def _res_mm_kernel(a_ref, w_ref, r_ref, o_ref, ob_ref, rstd_ref, ss_ref, *, tm, nk, d):
    j = pl.program_id(1)
    sub = min(RES_SUB_ROWS, tm)
    row_blocks = [slice(r * sub, (r + 1) * sub) for r in range(tm // sub)]

    def product(rows):
        return jnp.dot(a_ref[rows, :], w_ref[...], preferred_element_type=F32)

    def finish(rows, x_new):
        o_ref[rows, :] = x_new
        ob_ref[rows, :] = x_new.astype(ob_ref.dtype)
        part = jnp.sum(x_new * x_new, axis=-1, keepdims=True)
        ss = jnp.where(j == 0, part, ss_ref[rows, :] + part)
        ss_ref[rows, :] = ss
        rstd_ref[rows, :] = _rstd(ss, d)

    if nk == 1:
        for rows in row_blocks:
            finish(rows, r_ref[rows, :] + product(rows))
    else:
        k = pl.program_id(2)

        @pl.when(k == 0)
        def _():
            for rows in row_blocks:
                o_ref[rows, :] = r_ref[rows, :] + product(rows)

        if nk > 2:
            @pl.when(jnp.logical_and(k > 0, k < nk - 1))
            def _():
                for rows in row_blocks:
                    o_ref[rows, :] += product(rows)

        @pl.when(k == nk - 1)
        def _():
            for rows in row_blocks:
                finish(rows, o_ref[rows, :] + product(rows))


def res_mm(a, w, resid, *, layer, tm, tn, tk):
    t, k = a.shape
    n = w.shape[2]
    tm, tn, tk = min(tm, t), min(tn, n), min(tk, k)
    assert t % tm == 0 and n % tn == 0 and k % tk == 0 and tm % min(RES_SUB_ROWS, tm) == 0
    nj, nk = n // tn, k // tk
    tile = pl.BlockSpec((tm, tn), lambda i, j, kk: (i, j))
    return pl.pallas_call(
        functools.partial(_res_mm_kernel, tm=tm, nk=nk, d=n),
        grid=(t // tm, nj, nk),
        in_specs=[pl.BlockSpec((tm, tk), lambda i, j, kk: (i, kk)),
                  pl.BlockSpec((None, tk, tn), lambda i, j, kk: (layer, kk, j)),
                  tile],
        out_specs=[tile, tile, pl.BlockSpec((tm, 1), lambda i, j, kk: (i, 0))],
        out_shape=[jax.ShapeDtypeStruct((t, n), F32), jax.ShapeDtypeStruct((t, n), BF16),
                   jax.ShapeDtypeStruct((t, 1), F32)],
        scratch_shapes=[pltpu.VMEM((tm, 1), F32)],
        compiler_params=_params(3),
        name="res_mm",
    )(a, w, resid)


def _sgu_kernel(u_ref, v_ref, ws_ref, bs_ref, gv_ref, o_ref, vn_ref, *, n_chunks, n_groups):
    for c in range(n_chunks):
        rows = slice(c * CHUNK, (c + 1) * CHUNK)
        vf = v_ref[rows, :].astype(F32)
        r = lax.rsqrt(jnp.mean(vf * vf, axis=-1, keepdims=True) + EPS)
        vn_ref[rows, :] = (vf * r * gv_ref[...]).astype(vn_ref.dtype)
    for g in range(n_groups):
        cols = slice(g * CHUNK, (g + 1) * CHUNK)
        rhs = jnp.concatenate(
            [vn_ref[c * CHUNK:(c + 1) * CHUNK, cols] for c in range(n_chunks)], axis=1)
        s = jnp.dot(ws_ref[g], rhs, preferred_element_type=F32)
        for c in range(n_chunks):
            rows = slice(c * CHUNK, (c + 1) * CHUNK)
            sc = s[:, c * CHUNK:(c + 1) * CHUNK] + bs_ref[:, cols]
            o_ref[rows, cols] = (u_ref[rows, cols].astype(F32) * sc).astype(o_ref.dtype)


def sgu_gate(u, v, w_s, b_full, g_v, *, tm):
    t, d = u.shape
    n_groups = w_s.shape[0]
    tm = min(tm, t)
    assert tm % CHUNK == 0 and t % tm == 0 and d == n_groups * CHUNK
    row_spec = pl.BlockSpec((tm, d), lambda i: (i, 0))
    return pl.pallas_call(
        functools.partial(_sgu_kernel, n_chunks=tm // CHUNK, n_groups=n_groups),
        grid=(t // tm,),
        in_specs=[row_spec, row_spec,
                  pl.BlockSpec((n_groups, CHUNK, CHUNK), lambda i: (0, 0, 0)),
                  pl.BlockSpec((CHUNK, d), lambda i: (0, 0)),
                  pl.BlockSpec((1, d), lambda i: (0, 0))],
        out_specs=row_spec,
        out_shape=jax.ShapeDtypeStruct((t, d), BF16),
        scratch_shapes=[pltpu.VMEM((tm, d), BF16)],
        compiler_params=_params(1),
        name="sgu_gate",
    )(u, v, w_s, b_full, g_v.reshape(1, d).astype(F32))


CONV_ROWS = 64
CONV_COLS = LANES
WIN_ROWS = CONV_ROWS + 2 * HALO
TAP0 = HALO - CONV_PAD
LN_ROWS = 64
LN_APPLY_ROWS = 16


def _conv_kernel(g_ref, gp_ref, gn_ref, w_ref, bdw_ref, gln_ref, bln_ref, o_ref,
                 ext_ref, c_ref, mu_ref, rs_ref, *, tm, d, seq):
    row0 = pl.program_id(0) * tm
    at_start = lax.rem(row0, seq) == 0
    at_end = lax.rem(row0 + tm, seq) == 0

    ext_ref[HALO:HALO + tm, :] = g_ref[...].astype(F32)
    ext_ref[0:HALO, :] = jnp.where(at_start, 0.0, gp_ref[...].astype(F32))
    ext_ref[HALO + tm:2 * HALO + tm, :] = jnp.where(at_end, 0.0, gn_ref[...].astype(F32))

    for cs in range(d // CONV_COLS):
        cols = slice(cs * CONV_COLS, (cs + 1) * CONV_COLS)

        def conv_step(rc, carry, cols=cols):
            base = pl.multiple_of(rc * CONV_ROWS, CONV_ROWS)
            win = ext_ref[pl.ds(base, WIN_ROWS), cols]
            acc = jnp.zeros((CONV_ROWS, CONV_COLS), F32)
            for b in range(SUBLANES):
                wb = win if b == 0 else pltpu.roll(win, WIN_ROWS - b, axis=0)
                for m in range(b, TAP0 + CONV_WIDTH, SUBLANES):
                    k = m - TAP0
                    if k >= 0:
                        acc = acc + w_ref[k:k + 1, cols] * wb[m - b:m - b + CONV_ROWS, :]
            c_ref[pl.ds(base, CONV_ROWS), cols] = acc + bdw_ref[:, cols]
            return carry

        lax.fori_loop(0, tm // CONV_ROWS, conv_step, 0)

    def rows_of(rc, n):
        return pl.ds(pl.multiple_of(rc * n, n), n)

    def mean_step(rc, carry):
        rows = rows_of(rc, LN_ROWS)
        mu_ref[rows, :] = jnp.mean(c_ref[rows, :], axis=-1, keepdims=True)
        return carry

    def var_step(rc, carry):
        rows = rows_of(rc, LN_ROWS)
        xc = c_ref[rows, :] - mu_ref[rows, :]
        rs_ref[rows, :] = lax.rsqrt(jnp.mean(xc * xc, axis=-1, keepdims=True) + EPS)
        return carry

    def apply_step(rc, carry):
        rows = rows_of(rc, LN_APPLY_ROWS)
        y = (c_ref[rows, :] - mu_ref[rows, :]) * rs_ref[rows, :]
        y = y * gln_ref[...] + bln_ref[...]
        o_ref[rows, :] = (y * _sigmoid(y)).astype(o_ref.dtype)
        return carry

    lax.fori_loop(0, tm // LN_ROWS, mean_step, 0)
    lax.fori_loop(0, tm // LN_ROWS, var_step, 0)
    lax.fori_loop(0, tm // LN_APPLY_ROWS, apply_step, 0)


def conv_module(g, w_dw, b_dw, g_ln, b_ln, *, tm, seq):
    t, d = g.shape
    tm = min(tm, seq)
    assert t % seq == 0 and seq % tm == 0
    assert tm % CONV_ROWS == 0 and tm % LN_ROWS == 0 and tm % HALO == 0 and d % CONV_COLS == 0
    hb = tm // HALO
    n_halo_blocks = t // HALO
    vec = lambda x: x.reshape(1, d).astype(F32)
    vec_spec = pl.BlockSpec((1, d), lambda i: (0, 0))
    return pl.pallas_call(
        functools.partial(_conv_kernel, tm=tm, d=d, seq=seq),
        grid=(t // tm,),
        in_specs=[pl.BlockSpec((tm, d), lambda i: (i, 0)),
                  pl.BlockSpec((HALO, d), lambda i: (jnp.maximum(i * hb - 1, 0), 0)),
                  pl.BlockSpec((HALO, d), lambda i: (jnp.minimum((i + 1) * hb, n_halo_blocks - 1), 0)),
                  pl.BlockSpec((CONV_WIDTH, d), lambda i: (0, 0)),
                  vec_spec, vec_spec, vec_spec],
        out_specs=pl.BlockSpec((tm, d), lambda i: (i, 0)),
        out_shape=jax.ShapeDtypeStruct((t, d), BF16),
        scratch_shapes=[pltpu.VMEM((tm + 2 * HALO, d), F32), pltpu.VMEM((tm, d), F32),
                        pltpu.VMEM((tm, 1), F32), pltpu.VMEM((tm, 1), F32)],
        compiler_params=_params(1),
        name="conv_module",
    )(g, g, g, w_dw.astype(F32), vec(b_dw), vec(g_ln), vec(b_ln))


N_MIXERS = 2


def _prepare_weights(norm_mix, norm_ffn, a_w_in, a_w_s, a_b_s, a_w_out,
                     b_w_pw1, b_b_pw1, b_w_pw2, ffn_w_gate, ffn_w_up, ffn_w_down):
    gain = lambda g: g.astype(F32)[:, :, None]
    return {
        "w_in": (gain(norm_mix[0::N_MIXERS]) * a_w_in).astype(BF16),
        "w_s": a_w_s.astype(BF16),
        "b_full": jnp.repeat(jnp.swapaxes(a_b_s, 1, 2).astype(F32), CHUNK, axis=2),
        "w_out": a_w_out.astype(BF16),
        "w_pw1": (gain(norm_mix[1::N_MIXERS]) * b_w_pw1).astype(BF16),
        "b_pw1": b_b_pw1.astype(F32),
        "w_pw2": b_w_pw2.astype(BF16),
        "w_gate": (gain(norm_ffn) * ffn_w_gate).astype(BF16),
        "w_up": (gain(norm_ffn) * ffn_w_up).astype(BF16),
        "w_down": ffn_w_down.astype(BF16),
    }


def _trunk(x, seq, depth, wts, norm_final, a_g_v, b_w_dw, b_b_dw, b_g_ln, b_b_ln):
    d = x.shape[1]
    d_ff = wts["w_gate"].shape[2]
    xb, rstd = enter_stream(x)
    for i in range(depth):
        j = i // N_MIXERS
        if i % N_MIXERS == 0:
            u, v = pair_mm(xb, rstd, wts["w_in"], wts["w_in"], None, layer=j, mode="gelu_pair",
                           n=d, w2_col0=d, tm=1024, tn=512)
            us = sgu_gate(u, v, wts["w_s"][j], wts["b_full"][j], a_g_v[j], tm=512)
            x, xb, rstd = res_mm(us, wts["w_out"], x, layer=j, tm=1024, tn=512, tk=d)
        else:
            g = pair_mm(xb, rstd, wts["w_pw1"], wts["w_pw1"], wts["b_pw1"][j][None, :], layer=j,
                        mode="glu", n=d, w2_col0=d, tm=1024, tn=512)
            c = conv_module(g, b_w_dw[j], b_b_dw[j], b_g_ln[j], b_b_ln[j], tm=256, seq=seq)
            x, xb, rstd = res_mm(c, wts["w_pw2"], x, layer=j, tm=1024, tn=512, tk=d)
        hh = pair_mm(xb, rstd, wts["w_gate"], wts["w_up"], None, layer=i, mode="swiglu",
                     n=d_ff, w2_col0=0, tm=2048, tn=256)
        x, xb, rstd = res_mm(hh, wts["w_down"], x, layer=i, tm=1024, tn=512, tk=d_ff // 2)
    return rmsnorm(x, norm_final, F32)


def kernel(x_prompt, x_sample, norm_mix, norm_ffn, norm_final, a_w_in, a_w_s, a_b_s, a_g_v, a_w_out, b_w_pw1, b_b_pw1, b_w_dw, b_b_dw, b_g_ln, b_b_ln, b_w_pw2, ffn_w_gate, ffn_w_up, ffn_w_down):
    wts = _prepare_weights(norm_mix, norm_ffn, a_w_in, a_w_s, a_b_s, a_w_out,
                           b_w_pw1, b_b_pw1, b_w_pw2, ffn_w_gate, ffn_w_up, ffn_w_down)
    outs = []
    for x in (x_prompt, x_sample):
        d = x.shape[-1]
        y = _trunk(x.reshape(-1, d), x.shape[1], norm_mix.shape[0], wts, norm_final,
                   a_g_v, b_w_dw, b_b_dw, b_g_ln, b_b_ln)
        outs.append(y.reshape(x.shape))
    return tuple(outs)
```

```python
import functools

import jax
import jax.numpy as jnp
from jax import lax
from jax.experimental import pallas as pl
from jax.experimental.pallas import tpu as pltpu

EPS = 1e-6
CHUNK = 128
CONV_WIDTH = 31
CONV_PAD = CONV_WIDTH // 2
HALO = 16
LANES = 128
SUBLANES = 8
VMEM_LIMIT_BYTES = 56 * 1024 * 1024

F32 = jnp.float32
BF16 = jnp.bfloat16


def _params(n_axes):
    return pltpu.CompilerParams(
        dimension_semantics=("arbitrary",) * n_axes,
        vmem_limit_bytes=VMEM_LIMIT_BYTES)


def _sigmoid(x):
    return 1.0 / (1.0 + jnp.exp(-x))


def _gelu(x):
    return 0.5 * x * (1.0 + lax.erf(x * (2.0 ** -0.5)))


def _rstd(sum_sq, d):
    return lax.rsqrt(sum_sq / d + EPS)


def _enter_kernel(x_ref, xb_ref, rstd_ref):
    x = x_ref[...]
    xb_ref[...] = x.astype(xb_ref.dtype)
    rstd_ref[...] = _rstd(jnp.sum(x * x, axis=-1, keepdims=True), x.shape[-1])


def enter_stream(x, tm=256):
    t, d = x.shape
    tm = min(tm, t)
    return pl.pallas_call(
        _enter_kernel,
        grid=(t // tm,),
        in_specs=[pl.BlockSpec((tm, d), lambda i: (i, 0))],
        out_specs=[pl.BlockSpec((tm, d), lambda i: (i, 0)),
                   pl.BlockSpec((tm, 1), lambda i: (i, 0))],
        out_shape=[jax.ShapeDtypeStruct((t, d), BF16), jax.ShapeDtypeStruct((t, 1), F32)],
        compiler_params=_params(1),
        name="enter_stream",
    )(x)


def _rmsnorm_kernel(x_ref, g_ref, o_ref):
    x = x_ref[...]
    r = lax.rsqrt(jnp.mean(x * x, axis=-1, keepdims=True) + EPS)
    o_ref[...] = (x * r * g_ref[...]).astype(o_ref.dtype)


def rmsnorm(x, g, out_dtype, tm=256):
    t, d = x.shape
    tm = min(tm, t)
    return pl.pallas_call(
        _rmsnorm_kernel,
        grid=(t // tm,),
        in_specs=[pl.BlockSpec((tm, d), lambda i: (i, 0)),
                  pl.BlockSpec((1, d), lambda i: (0, 0))],
        out_specs=pl.BlockSpec((tm, d), lambda i: (i, 0)),
        out_shape=jax.ShapeDtypeStruct((t, d), out_dtype),
        compiler_params=_params(1),
        name="rmsnorm",
    )(x, g.reshape(1, d).astype(F32))


PAIR_SUB_ROWS = 512


def _pair_mm_kernel(*refs, mode):
    if mode == "glu":
        a_ref, rstd_ref, w1_ref, w2_ref, b1_ref, b2_ref, *o_refs = refs
    else:
        a_ref, rstd_ref, w1_ref, w2_ref, *o_refs = refs
    tm = a_ref.shape[0]
    sub = min(PAIR_SUB_ROWS, tm)
    for r in range(tm // sub):
        rows = slice(r * sub, (r + 1) * sub)
        a = a_ref[rows, :]
        rstd = rstd_ref[rows, :]
        z1 = jnp.dot(a, w1_ref[...], preferred_element_type=F32) * rstd
        z2 = jnp.dot(a, w2_ref[...], preferred_element_type=F32) * rstd
        if mode == "gelu_pair":
            o_refs[0][rows, :] = _gelu(z1).astype(o_refs[0].dtype)
            o_refs[1][rows, :] = _gelu(z2).astype(o_refs[1].dtype)
        elif mode == "glu":
            z1 = z1 + b1_ref[...]
            z2 = z2 + b2_ref[...]
            o_refs[0][rows, :] = (z1 * _sigmoid(z2)).astype(o_refs[0].dtype)
        elif mode == "swiglu":
            o_refs[0][rows, :] = (z1 * _sigmoid(z1) * z2).astype(o_refs[0].dtype)
        else:
            raise ValueError(mode)


def pair_mm(a, rstd, w1, w2, bias, *, layer, mode, n, w2_col0, tm, tn):
    t, k = a.shape
    tm = min(tm, t)
    tn = min(tn, n)
    assert t % tm == 0 and n % tn == 0 and w2_col0 % tn == 0
    off = w2_col0 // tn
    n_out = 2 if mode == "gelu_pair" else 1
    in_specs = [pl.BlockSpec((tm, k), lambda i, j: (i, 0)),
                pl.BlockSpec((tm, 1), lambda i, j: (i, 0)),
                pl.BlockSpec((None, k, tn), lambda i, j: (layer, 0, j)),
                pl.BlockSpec((None, k, tn), lambda i, j: (layer, 0, j + off))]
    args = [a, rstd, w1, w2]
    if mode == "glu":
        in_specs += [pl.BlockSpec((1, tn), lambda i, j: (0, j)),
                     pl.BlockSpec((1, tn), lambda i, j: (0, j + off))]
        args += [bias, bias]
    out_spec = pl.BlockSpec((tm, tn), lambda i, j: (i, j))
    out_shape = jax.ShapeDtypeStruct((t, n), BF16)
    res = pl.pallas_call(
        functools.partial(_pair_mm_kernel, mode=mode),
        grid=(t // tm, n // tn),
        in_specs=in_specs,
        out_specs=[out_spec] * n_out,
        out_shape=[out_shape] * n_out,
        compiler_params=_params(2),
        name="pair_mm_" + mode,
    )(*args)
    return res if n_out == 2 else res[0]


RES_SUB_ROWS = 256


def _res_mm_kernel(a_ref, w_ref, r_ref, o_ref, ob_ref, rstd_ref, ss_ref, *, tm, nk, d):
    j = pl.program_id(1)
    sub = min(RES_SUB_ROWS, tm)
    row_blocks = [slice(r * sub, (r + 1) * sub) for r in range(tm // sub)]

    def product(rows):
        return jnp.dot(a_ref[rows, :], w_ref[...], preferred_element_type=F32)

    def finish(rows, x_new):
        o_ref[rows, :] = x_new
        ob_ref[rows, :] = x_new.astype(ob_ref.dtype)
        part = jnp.sum(x_new * x_new, axis=-1, keepdims=True)
        ss = jnp.where(j == 0, part, ss_ref[rows, :] + part)
        ss_ref[rows, :] = ss
        rstd_ref[rows, :] = _rstd(ss, d)

    if nk == 1:
        for rows in row_blocks:
            finish(rows, r_ref[rows, :] + product(rows))
    else:
        k = pl.program_id(2)

        @pl.when(k == 0)
        def _():
            for rows in row_blocks:
                o_ref[rows, :] = r_ref[rows, :] + product(rows)

        if nk > 2:
            @pl.when(jnp.logical_and(k > 0, k < nk - 1))
            def _():
                for rows in row_blocks:
                    o_ref[rows, :] += product(rows)

        @pl.when(k == nk - 1)
        def _():
            for rows in row_blocks:
                finish(rows, o_ref[rows, :] + product(rows))


def res_mm(a, w, resid, *, layer, tm, tn, tk):
    t, k = a.shape
    n = w.shape[2]
    tm, tn, tk = min(tm, t), min(tn, n), min(tk, k)
    assert t % tm == 0 and n % tn == 0 and k % tk == 0 and tm % min(RES_SUB_ROWS, tm) == 0
    nj, nk = n // tn, k // tk
    tile = pl.BlockSpec((tm, tn), lambda i, j, kk: (i, j))
    return pl.pallas_call(
        functools.partial(_res_mm_kernel, tm=tm, nk=nk, d=n),
        grid=(t // tm, nj, nk),
        in_specs=[pl.BlockSpec((tm, tk), lambda i, j, kk: (i, kk)),
                  pl.BlockSpec((None, tk, tn), lambda i, j, kk: (layer, kk, j)),
                  tile],
        out_specs=[tile, tile, pl.BlockSpec((tm, 1), lambda i, j, kk: (i, 0))],
        out_shape=[jax.ShapeDtypeStruct((t, n), F32), jax.ShapeDtypeStruct((t, n), BF16),
                   jax.ShapeDtypeStruct((t, 1), F32)],
        scratch_shapes=[pltpu.VMEM((tm, 1), F32)],
        compiler_params=_params(3),
        name="res_mm",
    )(a, w, resid)


def _sgu_kernel(u_ref, v_ref, ws_ref, bs_ref, gv_ref, o_ref, vn_ref, *, n_chunks, n_groups):
    for c in range(n_chunks):
        rows = slice(c * CHUNK, (c + 1) * CHUNK)
        vf = v_ref[rows, :].astype(F32)
        r = lax.rsqrt(jnp.mean(vf * vf, axis=-1, keepdims=True) + EPS)
        vn_ref[rows, :] = (vf * r * gv_ref[...]).astype(vn_ref.dtype)
    for g in range(n_groups):
        cols = slice(g * CHUNK, (g + 1) * CHUNK)
        rhs = jnp.concatenate(
            [vn_ref[c * CHUNK:(c + 1) * CHUNK, cols] for c in range(n_chunks)], axis=1)
        s = jnp.dot(ws_ref[g], rhs, preferred_element_type=F32)
        for c in range(n_chunks):
            rows = slice(c * CHUNK, (c + 1) * CHUNK)
            sc = s[:, c * CHUNK:(c + 1) * CHUNK] + bs_ref[:, cols]
            o_ref[rows, cols] = (u_ref[rows, cols].astype(F32) * sc).astype(o_ref.dtype)


def sgu_gate(u, v, w_s, b_full, g_v, *, tm):
    t, d = u.shape
    n_groups = w_s.shape[0]
    tm = min(tm, t)
    assert tm % CHUNK == 0 and t % tm == 0 and d == n_groups * CHUNK
    row_spec = pl.BlockSpec((tm, d), lambda i: (i, 0))
    return pl.pallas_call(
        functools.partial(_sgu_kernel, n_chunks=tm // CHUNK, n_groups=n_groups),
        grid=(t // tm,),
        in_specs=[row_spec, row_spec,
                  pl.BlockSpec((n_groups, CHUNK, CHUNK), lambda i: (0, 0, 0)),
                  pl.BlockSpec((CHUNK, d), lambda i: (0, 0)),
                  pl.BlockSpec((1, d), lambda i: (0, 0))],
        out_specs=row_spec,
        out_shape=jax.ShapeDtypeStruct((t, d), BF16),
        scratch_shapes=[pltpu.VMEM((tm, d), BF16)],
        compiler_params=_params(1),
        name="sgu_gate",
    )(u, v, w_s, b_full, g_v.reshape(1, d).astype(F32))


CONV_ROWS = 64
CONV_COLS = LANES
WIN_ROWS = CONV_ROWS + 2 * HALO
TAP0 = HALO - CONV_PAD
LN_ROWS = 64
LN_APPLY_ROWS = 16


def _conv_kernel(g_ref, gp_ref, gn_ref, w_ref, bdw_ref, gln_ref, bln_ref, o_ref,
                 ext_ref, c_ref, mu_ref, rs_ref, *, tm, d, seq):
    row0 = pl.program_id(0) * tm
    at_start = lax.rem(row0, seq) == 0
    at_end = lax.rem(row0 + tm, seq) == 0

    ext_ref[HALO:HALO + tm, :] = g_ref[...].astype(F32)
    ext_ref[0:HALO, :] = jnp.where(at_start, 0.0, gp_ref[...].astype(F32))
    ext_ref[HALO + tm:2 * HALO + tm, :] = jnp.where(at_end, 0.0, gn_ref[...].astype(F32))

    for cs in range(d // CONV_COLS):
        cols = slice(cs * CONV_COLS, (cs + 1) * CONV_COLS)

        def conv_step(rc, carry, cols=cols):
            base = pl.multiple_of(rc * CONV_ROWS, CONV_ROWS)
            win = ext_ref[pl.ds(base, WIN_ROWS), cols]
            acc = jnp.zeros((CONV_ROWS, CONV_COLS), F32)
            for b in range(SUBLANES):
                wb = win if b == 0 else pltpu.roll(win, WIN_ROWS - b, axis=0)
                for m in range(b, TAP0 + CONV_WIDTH, SUBLANES):
                    k = m - TAP0
                    if k >= 0:
                        acc = acc + w_ref[k:k + 1, cols] * wb[m - b:m - b + CONV_ROWS, :]
            c_ref[pl.ds(base, CONV_ROWS), cols] = acc + bdw_ref[:, cols]
            return carry

        lax.fori_loop(0, tm // CONV_ROWS, conv_step, 0)

    def rows_of(rc, n):
        return pl.ds(pl.multiple_of(rc * n, n), n)

    def mean_step(rc, carry):
        rows = rows_of(rc, LN_ROWS)
        mu_ref[rows, :] = jnp.mean(c_ref[rows, :], axis=-1, keepdims=True)
        return carry

    def var_step(rc, carry):
        rows = rows_of(rc, LN_ROWS)
        xc = c_ref[rows, :] - mu_ref[rows, :]
        rs_ref[rows, :] = lax.rsqrt(jnp.mean(xc * xc, axis=-1, keepdims=True) + EPS)
        return carry

    def apply_step(rc, carry):
        rows = rows_of(rc, LN_APPLY_ROWS)
        y = (c_ref[rows, :] - mu_ref[rows, :]) * rs_ref[rows, :]
        y = y * gln_ref[...] + bln_ref[...]
        o_ref[rows, :] = (y * _sigmoid(y)).astype(o_ref.dtype)
        return carry

    lax.fori_loop(0, tm // LN_ROWS, mean_step, 0)
    lax.fori_loop(0, tm // LN_ROWS, var_step, 0)
    lax.fori_loop(0, tm // LN_APPLY_ROWS, apply_step, 0)


def conv_module(g, w_dw, b_dw, g_ln, b_ln, *, tm, seq):
    t, d = g.shape
    tm = min(tm, seq)
    assert t % seq == 0 and seq % tm == 0
    assert tm % CONV_ROWS == 0 and tm % LN_ROWS == 0 and tm % HALO == 0 and d % CONV_COLS == 0
    hb = tm // HALO
    n_halo_blocks = t // HALO
    vec = lambda x: x.reshape(1, d).astype(F32)
    vec_spec = pl.BlockSpec((1, d), lambda i: (0, 0))
    return pl.pallas_call(
        functools.partial(_conv_kernel, tm=tm, d=d, seq=seq),
        grid=(t // tm,),
        in_specs=[pl.BlockSpec((tm, d), lambda i: (i, 0)),
                  pl.BlockSpec((HALO, d), lambda i: (jnp.maximum(i * hb - 1, 0), 0)),
                  pl.BlockSpec((HALO, d), lambda i: (jnp.minimum((i + 1) * hb, n_halo_blocks - 1), 0)),
                  pl.BlockSpec((CONV_WIDTH, d), lambda i: (0, 0)),
                  vec_spec, vec_spec, vec_spec],
        out_specs=pl.BlockSpec((tm, d), lambda i: (i, 0)),
        out_shape=jax.ShapeDtypeStruct((t, d), BF16),
        scratch_shapes=[pltpu.VMEM((tm + 2 * HALO, d), F32), pltpu.VMEM((tm, d), F32),
                        pltpu.VMEM((tm, 1), F32), pltpu.VMEM((tm, 1), F32)],
        compiler_params=_params(1),
        name="conv_module",
    )(g, g, g, w_dw.astype(F32), vec(b_dw), vec(g_ln), vec(b_ln))


N_MIXERS = 2


def _prepare_weights(norm_mix, norm_ffn, a_w_in, a_w_s, a_b_s, a_w_out,
                     b_w_pw1, b_b_pw1, b_w_pw2, ffn_w_gate, ffn_w_up, ffn_w_down):
    gain = lambda g: g.astype(F32)[:, :, None]
    return {
        "w_in": (gain(norm_mix[0::N_MIXERS]) * a_w_in).astype(BF16),
        "w_s": a_w_s.astype(BF16),
        "b_full": jnp.repeat(jnp.swapaxes(a_b_s, 1, 2).astype(F32), CHUNK, axis=2),
        "w_out": a_w_out.astype(BF16),
        "w_pw1": (gain(norm_mix[1::N_MIXERS]) * b_w_pw1).astype(BF16),
        "b_pw1": b_b_pw1.astype(F32),
        "w_pw2": b_w_pw2.astype(BF16),
        "w_gate": (gain(norm_ffn) * ffn_w_gate).astype(BF16),
        "w_up": (gain(norm_ffn) * ffn_w_up).astype(BF16),
        "w_down": ffn_w_down.astype(BF16),
    }


def _trunk(x, seq, depth, wts, norm_final, a_g_v, b_w_dw, b_b_dw, b_g_ln, b_b_ln):
    d = x.shape[1]
    d_ff = wts["w_gate"].shape[2]
    xb, rstd = enter_stream(x)
    for i in range(depth):
        j = i // N_MIXERS
        if i % N_MIXERS == 0:
            u, v = pair_mm(xb, rstd, wts["w_in"], wts["w_in"], None, layer=j, mode="gelu_pair",
                           n=d, w2_col0=d, tm=1024, tn=512)
            us = sgu_gate(u, v, wts["w_s"][j], wts["b_full"][j], a_g_v[j], tm=512)
            x, xb, rstd = res_mm(us, wts["w_out"], x, layer=j, tm=1024, tn=512, tk=d)
        else:
            g = pair_mm(xb, rstd, wts["w_pw1"], wts["w_pw1"], wts["b_pw1"][j][None, :], layer=j,
                        mode="glu", n=d, w2_col0=d, tm=1024, tn=512)
            c = conv_module(g, b_w_dw[j], b_b_dw[j], b_g_ln[j], b_b_ln[j], tm=256, seq=seq)
            x, xb, rstd = res_mm(c, wts["w_pw2"], x, layer=j, tm=1024, tn=512, tk=d)
        hh = pair_mm(xb, rstd, wts["w_gate"], wts["w_up"], None, layer=i, mode="swiglu",
                     n=d_ff, w2_col0=0, tm=2048, tn=256)
        x, xb, rstd = res_mm(hh, wts["w_down"], x, layer=i, tm=512, tn=512, tk=d_ff)
    return rmsnorm(x, norm_final, F32)


def kernel(x_prompt, x_sample, norm_mix, norm_ffn, norm_final, a_w_in, a_w_s, a_b_s, a_g_v, a_w_out, b_w_pw1, b_b_pw1, b_w_dw, b_b_dw, b_g_ln, b_b_ln, b_w_pw2, ffn_w_gate, ffn_w_up, ffn_w_down):
    wts = _prepare_weights(norm_mix, norm_ffn, a_w_in, a_w_s, a_b_s, a_w_out,
                           b_w_pw1, b_b_pw1, b_w_pw2, ffn_w_gate, ffn_w_up, ffn_w_down)
    outs = []
    for x in (x_prompt, x_sample):
        d = x.shape[-1]
        y = _trunk(x.reshape(-1, d), x.shape[1], norm_mix.shape[0], wts, norm_final,
                   a_g_v, b_w_dw, b_b_dw, b_g_ln, b_b_ln)
        outs.append(y.reshape(x.shape))
    return tuple(outs)
```
